```python
import jax, jax.numpy as jnp
from jax import lax
import numpy as np

D_MODEL = 4096
BATCH = 2
SEQ = 8192
DEPTH = 1

CHUNK = 64

RMS_EPS = 1e-6
LRU_WIDTH = D_MODEL
LRU_HEADS = 16
LRU_HEAD_DIM = LRU_WIDTH // LRU_HEADS
CONV_WIDTH = 4
LRU_C = 8.0
POOL_WINDOWS = (2, 4, 8, 16)
POOL_GROUPS = len(POOL_WINDOWS)
POOL_WIDTH = D_MODEL
POOL_GROUP_DIM = POOL_WIDTH // POOL_GROUPS
N_BRANCHES = 2
IN_WIDTH = 2 * LRU_WIDTH + POOL_WIDTH + N_BRANCHES * D_MODEL
D_FF = -(-8 * D_MODEL // (3 * 256)) * 256
N_MOD = 6

kernel_name = "hybrid_rglru_pool_swiglu_block"


def _rmsnorm(x, g):
    x32 = x.astype(jnp.float32)
    y = x32 * lax.rsqrt(jnp.mean(x32 * x32, axis=-1, keepdims=True) + RMS_EPS)
    return y.astype(x.dtype) * g


def _modulate(u, shift, scale):
    return u * (1 + scale[:, None, :]) + shift[:, None, :]


def _causal_depthwise_conv(x, w, b):
    S = x.shape[1]
    xp = jnp.pad(x, ((0, 0), (CONV_WIDTH - 1, 0), (0, 0)))
    y = b
    for k in range(CONV_WIDTH):
        y = y + w[k] * xp[:, k:k + S]
    return y


def _block_diag_linear(x, w, b):
    B, S, _ = x.shape
    xh = x.reshape(B, S, LRU_HEADS, LRU_HEAD_DIM)
    return jnp.einsum('bshi,hij->bshj', xh, w).reshape(B, S, LRU_WIDTH) + b


def _rg_lru(x, w_a, b_a, w_x, b_x, lam):
    B, S, _ = x.shape
    r = jax.nn.sigmoid(_block_diag_linear(x, w_a, b_a).astype(jnp.float32))
    i = jax.nn.sigmoid(_block_diag_linear(x, w_x, b_x).astype(jnp.float32))
    log_a = -LRU_C * r * jax.nn.softplus(-lam.astype(jnp.float32))
    a = jnp.exp(log_a)
    u = jnp.sqrt(-jnp.expm1(2.0 * log_a)) * (i * x.astype(jnp.float32))

    def step(h, inp):
        a_t, u_t = inp
        h = a_t * h + u_t
        return h, h

    h0 = jnp.zeros((B, LRU_WIDTH), jnp.float32)
    _, hs = lax.scan(step, h0, (jnp.swapaxes(a, 0, 1), jnp.swapaxes(u, 0, 1)))
    return jnp.swapaxes(hs, 0, 1).astype(x.dtype)


def _multiscale_pool(p, pool_w, pool_scale):
    B, S, _ = p.shape
    p32 = p.astype(jnp.float32).reshape(B, S, POOL_GROUPS, POOL_GROUP_DIM)
    cs = jnp.cumsum(p32, axis=1)
    t = jnp.arange(S)
    outs = []
    for g, w in enumerate(POOL_WINDOWS):
        csg = cs[:, :, g]
        prev = jnp.pad(csg, ((0, 0), (w, 0), (0, 0)))[:, :S]
        cnt = jnp.minimum(t + 1, w).astype(jnp.float32)[None, :, None]
        outs.append((csg - prev) / cnt)
    pooled = (jnp.stack(outs, axis=2) - p32).astype(p.dtype)
    mixed = jnp.einsum('bsgi,gij->bsgj', pooled, pool_w).reshape(B, S, POOL_WIDTH)
    return mixed * pool_scale


def _normal(k, shape, fan_in):
    return jax.random.normal(k, shape, jnp.float32) * (fan_in ** -0.5)


def setup_inputs(seed: int = 0) -> dict:
    key = jax.random.key(seed)
    ks = jax.random.split(key, 24)
    L = DEPTH
    u = jax.random.uniform(ks[11], (L, LRU_WIDTH), jnp.float32, 0.9, 0.999)
    a0 = u ** (1.0 / LRU_C)
    lru_lambda = jnp.log(a0) - jnp.log1p(-a0)
    return {
        "x": jax.random.normal(ks[0], (BATCH, SEQ, D_MODEL), jnp.float32),
        "c": jax.random.normal(ks[1], (BATCH, D_MODEL), jnp.float32),
        "w_ada": _normal(ks[2], (L, D_MODEL, N_MOD * D_MODEL), D_MODEL),
        "b_ada": 0.02 * jax.random.normal(ks[3], (L, N_MOD * D_MODEL), jnp.float32),
        "g_mix_pre": 1.0 + 0.05 * jax.random.normal(ks[4], (L, D_MODEL), jnp.float32),
        "g_mix_post": 1.0 + 0.05 * jax.random.normal(ks[5], (L, D_MODEL), jnp.float32),
        "w_in": _normal(ks[6], (L, D_MODEL, IN_WIDTH), D_MODEL),
        "conv_w": _normal(ks[7], (L, CONV_WIDTH, LRU_WIDTH), CONV_WIDTH),
        "conv_b": 0.02 * jax.random.normal(ks[8], (L, LRU_WIDTH), jnp.float32),
        "w_rg_a": _normal(ks[9], (L, LRU_HEADS, LRU_HEAD_DIM, LRU_HEAD_DIM), LRU_HEAD_DIM),
        "b_rg_a": 0.02 * jax.random.normal(ks[10], (L, LRU_WIDTH), jnp.float32),
        "w_rg_x": _normal(ks[12], (L, LRU_HEADS, LRU_HEAD_DIM, LRU_HEAD_DIM), LRU_HEAD_DIM),
        "b_rg_x": 0.02 * jax.random.normal(ks[13], (L, LRU_WIDTH), jnp.float32),
        "lru_lambda": lru_lambda,
        "pool_w": _normal(ks[14], (L, POOL_GROUPS, POOL_GROUP_DIM, POOL_GROUP_DIM), POOL_GROUP_DIM),
        "pool_scale": 1.0 + 0.1 * jax.random.normal(ks[15], (L, POOL_WIDTH), jnp.float32),
        "w_branch_lru": _normal(ks[16], (L, LRU_WIDTH, D_MODEL), LRU_WIDTH),
        "w_branch_pool": _normal(ks[17], (L, POOL_WIDTH, D_MODEL), POOL_WIDTH),
        "w_o": _normal(ks[18], (L, D_MODEL, D_MODEL), D_MODEL),
        "g_ffn_pre": 1.0 + 0.05 * jax.random.normal(ks[19], (L, D_MODEL), jnp.float32),
        "g_ffn_post": 1.0 + 0.05 * jax.random.normal(ks[20], (L, D_MODEL), jnp.float32),
        "w_gate_up": _normal(ks[21], (L, D_MODEL, 2 * D_FF), D_MODEL),
        "w_down": _normal(ks[22], (L, D_FF, D_MODEL), D_FF),
    }


def reference(x, c, w_ada, b_ada, g_mix_pre, g_mix_post, w_in, conv_w, conv_b,
              w_rg_a, b_rg_a, w_rg_x, b_rg_x, lru_lambda, pool_w, pool_scale,
              w_branch_lru, w_branch_pool, w_o, g_ffn_pre, g_ffn_post, w_gate_up, w_down):
    splits = [LRU_WIDTH, 2 * LRU_WIDTH, 2 * LRU_WIDTH + POOL_WIDTH,
              2 * LRU_WIDTH + POOL_WIDTH + D_MODEL]
    c_act = jax.nn.silu(c)
    for l in range(DEPTH):
        mod = c_act @ w_ada[l] + b_ada[l]
        sh_m, sc_m, gt_m, sh_f, sc_f, gt_f = jnp.split(mod, N_MOD, axis=-1)

        u = _modulate(_rmsnorm(x, g_mix_pre[l]), sh_m, sc_m)
        proj = u @ w_in[l]
        xr, gr, xp, m_lru, m_pool = jnp.split(proj, splits, axis=-1)
        xr = _causal_depthwise_conv(xr, conv_w[l], conv_b[l])
        y_lru = _rg_lru(xr, w_rg_a[l], b_rg_a[l], w_rg_x[l], b_rg_x[l], lru_lambda[l]) * jax.nn.gelu(gr)
        y_pool = _multiscale_pool(xp, pool_w[l], pool_scale[l])
        merged = (jax.nn.sigmoid(m_lru) * (y_lru @ w_branch_lru[l])
                  + jax.nn.sigmoid(m_pool) * (y_pool @ w_branch_pool[l]))
        y = merged @ w_o[l]
        x = x + gt_m[:, None, :] * _rmsnorm(y, g_mix_post[l])

        u = _modulate(_rmsnorm(x, g_ffn_pre[l]), sh_f, sc_f)
        gate, up = jnp.split(u @ w_gate_up[l], 2, axis=-1)
        y = (jax.nn.silu(gate) * up) @ w_down[l]
        x = x + gt_f[:, None, :] * _rmsnorm(y, g_ffn_post[l])
    return x
```

```python
import functools

import jax
import jax.numpy as jnp
from jax import lax
from jax.experimental import pallas as pl
from jax.experimental.pallas import tpu as pltpu

F32 = jnp.float32
BF16 = jnp.bfloat16

RMS_EPS = 1e-6
LRU_C = 8.0
LRU_HEAD_DIM = 256
CONV_WIDTH = 4
POOL_WINDOWS = (2, 4, 8, 16)

SUBLANES = 8
MIB = 1024 * 1024

TM = 512
TN_IN = 1024
TN_MERGE = 512
TK_ACC = 512
TF = 512
TL = 256
LRU_CW = 512
TN_ADA = 512
POOL_HALO = 32


def _rmsnorm_rows(x, g):
    ms = jnp.mean(x * x, axis=-1, keepdims=True)
    return x * lax.rsqrt(ms + RMS_EPS) * g


def _sigmoid(x):
    return 1.0 / (1.0 + jnp.exp(-x))


def _gelu_tanh(x):
    return 0.5 * x * (1.0 + jnp.tanh(0.7978845608028654 * (x + 0.044715 * (x * x * x))))


def _params(sem, vmem_mib):
    return pltpu.CompilerParams(dimension_semantics=sem, vmem_limit_bytes=vmem_mib * MIB)


def _ada_kernel(c_ref, w_ref, b_ref, o_ref):
    c = c_ref[...]
    ca = (c * _sigmoid(c)).astype(BF16)
    o_ref[...] = jnp.dot(ca, w_ref[...].astype(BF16), preferred_element_type=F32) + b_ref[...]


def _ada(c_pad, w_ada, b_ada):
    d, n = w_ada.shape
    return pl.pallas_call(
        _ada_kernel,
        grid=(n // TN_ADA,),
        in_specs=[
            pl.BlockSpec((SUBLANES, d), lambda j: (0, 0)),
            pl.BlockSpec((d, TN_ADA), lambda j: (0, j)),
            pl.BlockSpec((1, TN_ADA), lambda j: (0, j)),
        ],
        out_specs=pl.BlockSpec((SUBLANES, TN_ADA), lambda j: (0, j)),
        out_shape=jax.ShapeDtypeStruct((SUBLANES, n), F32),
        compiler_params=_params(("arbitrary",), 32),
        name="ada",
    )(c_pad, w_ada, b_ada)


def _prenorm_mm_kernel(x_ref, mod_ref, g_ref, w_ref, o_ref, u_ref, *, shift_row, swiglu):
    @pl.when(pl.program_id(1) == 0)
    def _():
        y = _rmsnorm_rows(x_ref[...], g_ref[...])
        sh = mod_ref[0, shift_row:shift_row + 1, :]
        sc = mod_ref[0, shift_row + 1:shift_row + 2, :]
        u_ref[...] = (y * (1.0 + sc) + sh).astype(BF16)

    r = jnp.dot(u_ref[...], w_ref[...], preferred_element_type=F32)
    if swiglu:
        half = r.shape[1] // 2
        gate = r[:, :half]
        r = gate * _sigmoid(gate) * r[:, half:]
    o_ref[...] = r.astype(o_ref.dtype)


def _prenorm_mm(x2d, modv, g, w, *, seq, tn, shift_row, swiglu, out_dtype, vmem_mib):
    t, d = x2d.shape
    n = w.shape[1]
    tiles_per_seq = seq // TM
    n_out = n // 2 if swiglu else n
    tn_out = tn // 2 if swiglu else tn
    return pl.pallas_call(
        functools.partial(_prenorm_mm_kernel, shift_row=shift_row, swiglu=swiglu),
        grid=(t // TM, n // tn),
        in_specs=[
            pl.BlockSpec((TM, d), lambda i, j: (i, 0)),
            pl.BlockSpec((1, SUBLANES, d), lambda i, j: (i // tiles_per_seq, 0, 0)),
            pl.BlockSpec((1, d), lambda i, j: (0, 0)),
            pl.BlockSpec((d, tn), lambda i, j: (0, j)),
        ],
        out_specs=pl.BlockSpec((TM, tn_out), lambda i, j: (i, j)),
        out_shape=jax.ShapeDtypeStruct((t, n_out), out_dtype),
        scratch_shapes=[pltpu.VMEM((TM, d), BF16)],
        compiler_params=_params(("arbitrary", "arbitrary"), vmem_mib),
        name="gate_up" if swiglu else "inproj",
    )(x2d, modv, g, w)


def _lru_kernel(xr_ref, gr_ref, cw_ref, cb_ref, wa_ref, ba_ref, wx_ref, bx_ref, lam_ref,
                o_ref, ext_ref, a_ref, u_ref, h_ref):
    tl, d = xr_ref.shape
    halo = SUBLANES
    t = pl.program_id(1)

    @pl.when(t == 0)
    def _():
        ext_ref[0:halo, :] = jnp.zeros((halo, d), F32)
        h_ref[...] = jnp.zeros(h_ref.shape, F32)

    @pl.when(t > 0)
    def _():
        ext_ref[0:halo, :] = ext_ref[tl:tl + halo, :]

    ext_ref[halo:, :] = xr_ref[...]

    rowid = lax.broadcasted_iota(jnp.int32, (SUBLANES, LRU_CW), 0)
    heads_per_chunk = LRU_CW // LRU_HEAD_DIM

    for c in range(d // LRU_CW):
        cs = slice(c * LRU_CW, (c + 1) * LRU_CW)
        xc = cb_ref[:, cs]
        for k in range(CONV_WIDTH):
            off = halo - (CONV_WIDTH - 1 - k)
            xc = xc + cw_ref[k:k + 1, cs] * ext_ref[off:off + tl, cs]
        xcb = xc.astype(BF16)
        la, lx = [], []
        for hh in range(heads_per_chunk):
            head = c * heads_per_chunk + hh
            xh = xcb[:, hh * LRU_HEAD_DIM:(hh + 1) * LRU_HEAD_DIM]
            la.append(jnp.dot(xh, wa_ref[head], preferred_element_type=F32))
            lx.append(jnp.dot(xh, wx_ref[head], preferred_element_type=F32))
        r = _sigmoid(jnp.concatenate(la, axis=1) + ba_ref[:, cs])
        i = _sigmoid(jnp.concatenate(lx, axis=1) + bx_ref[:, cs])
        z = -lam_ref[:, cs]
        softplus = jnp.maximum(z, 0.0) + jnp.log1p(jnp.exp(-jnp.abs(z)))
        log_a = (-LRU_C * softplus) * r
        a = jnp.exp(log_a)
        a_ref[...] = a
        u_ref[...] = jnp.sqrt(1.0 - a * a) * (i * xc)

        def blk(kk, hprev):
            r0 = pl.multiple_of(kk * SUBLANES, SUBLANES)
            a8 = a_ref[pl.ds(r0, SUBLANES), :]
            u8 = u_ref[pl.ds(r0, SUBLANES), :]
            for s in (1, 2, 4):
                a_sh = pltpu.roll(a8, s, 0)
                u_sh = pltpu.roll(u8, s, 0)
                m = rowid >= s
                u8 = jnp.where(m, a8 * u_sh + u8, u8)
                a8 = jnp.where(m, a8 * a_sh, a8)
            h8 = a8 * hprev + u8
            u_ref[pl.ds(r0, SUBLANES), :] = h8
            return h8[SUBLANES - 1:SUBLANES, :]

        h_last = lax.fori_loop(0, tl // SUBLANES, blk, h_ref[0:1, cs], unroll=4)
        h_ref[0:1, cs] = h_last
        o_ref[:, cs] = (u_ref[...] * _gelu_tanh(gr_ref[:, cs])).astype(o_ref.dtype)


def _lru(proj, conv_w, conv_b, wa, ba, wx, bx, lam, *, batch, seq, d):
    nt = seq // TL
    full2 = lambda b, t: (0, 0)
    full3 = lambda b, t: (0, 0, 0)
    nh = wa.shape[0]
    return pl.pallas_call(
        _lru_kernel,
        grid=(batch, nt),
        in_specs=[
            pl.BlockSpec((TL, d), lambda b, t: (b * nt + t, 0)),
            pl.BlockSpec((TL, d), lambda b, t: (b * nt + t, 1)),
            pl.BlockSpec((CONV_WIDTH, d), full2),
            pl.BlockSpec((1, d), full2),
            pl.BlockSpec((nh, LRU_HEAD_DIM, LRU_HEAD_DIM), full3),
            pl.BlockSpec((1, d), full2),
            pl.BlockSpec((nh, LRU_HEAD_DIM, LRU_HEAD_DIM), full3),
            pl.BlockSpec((1, d), full2),
            pl.BlockSpec((1, d), full2),
        ],
        out_specs=pl.BlockSpec((TL, d), lambda b, t: (b * nt + t, 0)),
        out_shape=jax.ShapeDtypeStruct((batch * seq, d), BF16),
        scratch_shapes=[
            pltpu.VMEM((TL + SUBLANES, d), F32),
            pltpu.VMEM((TL, LRU_CW), F32),
            pltpu.VMEM((TL, LRU_CW), F32),
            pltpu.VMEM((SUBLANES, d), F32),
        ],
        compiler_params=_params(("arbitrary", "arbitrary"), 48),
        name="lru",
    )(proj, proj, conv_w, conv_b, wa, ba, wx, bx, lam)


def _pool_kernel(xp_ref, pw_ref, ps_ref, o_ref, ext_ref, sa_ref, sb_ref):
    tl, d = xp_ref.shape
    h = POOL_HALO
    rows = tl + h
    t = pl.program_id(1)
    gd = d // len(POOL_WINDOWS)

    @pl.when(t == 0)
    def _():
        ext_ref[0:h, :] = jnp.zeros((h, d), F32)

    @pl.when(t > 0)
    def _():
        ext_ref[0:h, :] = ext_ref[tl:tl + h, :]

    ext_ref[h:, :] = xp_ref[...]

    frame = t * tl + lax.broadcasted_iota(jnp.int32, (tl, 1), 0)

    for g, w in enumerate(POOL_WINDOWS):
        cs = slice(g * gd, (g + 1) * gd)
        cur = ext_ref[8:rows, cs] + ext_ref[7:rows - 1, cs]
        start = 8
        if w >= 4:
            sa_ref[8:rows, :] = cur
            cur = sa_ref[16:rows, :] + sa_ref[14:rows - 2, :]
            start = 16
        if w >= 8:
            sb_ref[16:rows, :] = cur
            cur = sb_ref[24:rows, :] + sb_ref[20:rows - 4, :]
            start = 24
        if w >= 16:
            sa_ref[24:rows, :] = cur
            cur = sa_ref[32:rows, :] + sa_ref[24:rows - 8, :]
            start = 32
        win = cur[h - start:, :]
        cnt = jnp.minimum(frame + 1, w).astype(F32)
        pooled = (win / cnt - ext_ref[h:rows, cs]).astype(BF16)
        mixed = jnp.dot(pooled, pw_ref[g], preferred_element_type=F32) * ps_ref[:, cs]
        o_ref[:, cs] = mixed.astype(o_ref.dtype)


def _pool(proj, pool_w, pool_scale, *, batch, seq, d):
    nt = seq // TL
    ng, gd, _ = pool_w.shape
    return pl.pallas_call(
        _pool_kernel,
        grid=(batch, nt),
        in_specs=[
            pl.BlockSpec((TL, d), lambda b, t: (b * nt + t, 2)),
            pl.BlockSpec((ng, gd, gd), lambda b, t: (0, 0, 0)),
            pl.BlockSpec((1, d), lambda b, t: (0, 0)),
        ],
        out_specs=pl.BlockSpec((TL, d), lambda b, t: (b * nt + t, 0)),
        out_shape=jax.ShapeDtypeStruct((batch * seq, d), BF16),
        scratch_shapes=[
            pltpu.VMEM((TL + POOL_HALO, d), F32),
            pltpu.VMEM((TL + POOL_HALO, gd), F32),
            pltpu.VMEM((TL + POOL_HALO, gd), F32),
        ],
        compiler_params=_params(("arbitrary", "arbitrary"), 48),
        name="pool",
    )(proj, pool_w, pool_scale)


def _merge_kernel(yl_ref, yp_ref, ml_ref, mp_ref, wl_ref, wp_ref, o_ref):
    bl = jnp.dot(yl_ref[...], wl_ref[...], preferred_element_type=F32)
    bp = jnp.dot(yp_ref[...], wp_ref[...], preferred_element_type=F32)
    o_ref[...] = (_sigmoid(ml_ref[...]) * bl + _sigmoid(mp_ref[...]) * bp).astype(o_ref.dtype)


def _merge(y_lru, y_pool, proj, wl, wp):
    t, d = y_lru.shape
    tn = TN_MERGE
    col0 = (proj.shape[1] - 2 * d) // tn
    return pl.pallas_call(
        _merge_kernel,
        grid=(t // TM, d // tn),
        in_specs=[
            pl.BlockSpec((TM, d), lambda i, j: (i, 0)),
            pl.BlockSpec((TM, d), lambda i, j: (i, 0)),
            pl.BlockSpec((TM, tn), lambda i, j: (i, col0 + j)),
            pl.BlockSpec((TM, tn), lambda i, j: (i, col0 + d // tn + j)),
            pl.BlockSpec((d, tn), lambda i, j: (0, j)),
            pl.BlockSpec((d, tn), lambda i, j: (0, j)),
        ],
        out_specs=pl.BlockSpec((TM, tn), lambda i, j: (i, j)),
        out_shape=jax.ShapeDtypeStruct((t, d), BF16),
        compiler_params=_params(("arbitrary", "arbitrary"), 48),
        name="merge",
    )(y_lru, y_pool, proj, proj, wl, wp)


def _panel_kernel(a_ref, w_ref, x_ref, mod_ref, g_ref, o_ref, *, gate_row):
    k = pl.program_id(1)

    @pl.when(k == 0)
    def _():
        o_ref[...] = jnp.zeros(o_ref.shape, F32)

    o_ref[...] += jnp.dot(a_ref[...], w_ref[...], preferred_element_type=F32)

    @pl.when(k == pl.num_programs(1) - 1)
    def _():
        gate = mod_ref[0, gate_row:gate_row + 1, :]
        o_ref[...] = x_ref[...] + gate * _rmsnorm_rows(o_ref[...], g_ref[...])


def _panel(a, w, x2d, modv, g, *, seq, gate_row, name):
    t, kdim = a.shape
    d = w.shape[1]
    tiles_per_seq = seq // TM
    return pl.pallas_call(
        functools.partial(_panel_kernel, gate_row=gate_row),
        grid=(t // TM, kdim // TK_ACC),
        in_specs=[
            pl.BlockSpec((TM, TK_ACC), lambda i, k: (i, k)),
            pl.BlockSpec((TK_ACC, d), lambda i, k: (k, 0)),
            pl.BlockSpec((TM, d), lambda i, k: (i, 0)),
            pl.BlockSpec((1, SUBLANES, d), lambda i, k: (i // tiles_per_seq, 0, 0)),
            pl.BlockSpec((1, d), lambda i, k: (0, 0)),
        ],
        out_specs=pl.BlockSpec((TM, d), lambda i, k: (i, 0)),
        out_shape=jax.ShapeDtypeStruct((t, d), F32),
        compiler_params=_params(("arbitrary", "arbitrary"), 58),
        name=name,
    )(a, w, x2d, modv, g)


def _layer(x2d, modv, *, batch, seq, g_mix_pre, g_mix_post, w_in, conv_w, conv_b, w_rg_a, b_rg_a,
           w_rg_x, b_rg_x, lru_lambda, pool_w, pool_scale, w_branch_lru, w_branch_pool, w_o,
           g_ffn_pre, g_ffn_post, w_gate_up, w_down):
    d = x2d.shape[1]
    row = lambda v: v.reshape(1, -1)

    proj = _prenorm_mm(x2d, modv, row(g_mix_pre), w_in.astype(BF16), seq=seq, tn=TN_IN,
                       shift_row=0, swiglu=False, out_dtype=F32, vmem_mib=56)
    y_lru = _lru(proj, conv_w, row(conv_b), w_rg_a.astype(BF16), row(b_rg_a),
                 w_rg_x.astype(BF16), row(b_rg_x), row(lru_lambda), batch=batch, seq=seq, d=d)
    y_pool = _pool(proj, pool_w.astype(BF16), row(pool_scale), batch=batch, seq=seq, d=d)
    merged = _merge(y_lru, y_pool, proj, w_branch_lru.astype(BF16), w_branch_pool.astype(BF16))
    x1 = _panel(merged, w_o.astype(BF16), x2d, modv, row(g_mix_post), seq=seq, gate_row=2,
                name="wo_post")

    f = w_down.shape[0]
    fp = -(-f // (2 * TF)) * (2 * TF)
    nf = fp // TF
    wg = jnp.pad(w_gate_up[:, :f], ((0, 0), (0, fp - f))).reshape(d, nf, TF)
    wu = jnp.pad(w_gate_up[:, f:], ((0, 0), (0, fp - f))).reshape(d, nf, TF)
    w_gu = jnp.stack([wg, wu], axis=2).reshape(d, 2 * fp).astype(BF16)
    w_dn = jnp.pad(w_down, ((0, fp - f), (0, 0))).astype(BF16)

    hid = _prenorm_mm(x1, modv, row(g_ffn_pre), w_gu, seq=seq, tn=2 * TF, shift_row=3,
                      swiglu=True, out_dtype=BF16, vmem_mib=56)
    return _panel(hid, w_dn, x1, modv, row(g_ffn_post), seq=seq, gate_row=5, name="down_post")


def kernel(x, c, w_ada, b_ada, g_mix_pre, g_mix_post, w_in, conv_w, conv_b, w_rg_a, b_rg_a,
           w_rg_x, b_rg_x, lru_lambda, pool_w, pool_scale, w_branch_lru, w_branch_pool, w_o,
           g_ffn_pre, g_ffn_post, w_gate_up, w_down):
    batch, seq, d = x.shape
    depth = w_ada.shape[0]
    n_mod = w_ada.shape[2] // d
    assert seq % TM == 0 and seq % TL == 0 and d % TN_IN == 0 and batch <= SUBLANES

    c_pad = jnp.pad(c, ((0, SUBLANES - batch), (0, 0)))
    x2d = x.reshape(batch * seq, d)
    for l in range(depth):
        mod = _ada(c_pad, w_ada[l], b_ada[l].reshape(1, -1))
        modv = jnp.pad(mod[:batch].reshape(batch, n_mod, d), ((0, 0), (0, SUBLANES - n_mod), (0, 0)))
        x2d = _layer(
            x2d, modv, batch=batch, seq=seq,
            g_mix_pre=g_mix_pre[l], g_mix_post=g_mix_post[l], w_in=w_in[l], conv_w=conv_w[l],
            conv_b=conv_b[l], w_rg_a=w_rg_a[l], b_rg_a=b_rg_a[l], w_rg_x=w_rg_x[l],
            b_rg_x=b_rg_x[l], lru_lambda=lru_lambda[l], pool_w=pool_w[l],
            pool_scale=pool_scale[l], w_branch_lru=w_branch_lru[l], w_branch_pool=w_branch_pool[l],
            w_o=w_o[l], g_ffn_pre=g_ffn_pre[l], g_ffn_post=g_ffn_post[l],
            w_gate_up=w_gate_up[l], w_down=w_down[l])
    return x2d.reshape(batch, seq, d)
```

```python
import functools

import jax
import jax.numpy as jnp
from jax import lax
from jax.experimental import pallas as pl
from jax.experimental.pallas import tpu as pltpu

F32 = jnp.float32
BF16 = jnp.bfloat16

RMS_EPS = 1e-6
LRU_C = 8.0
LRU_HEAD_DIM = 256
CONV_WIDTH = 4
POOL_WINDOWS = (2, 4, 8, 16)

SUBLANES = 8
MIB = 1024 * 1024

TM = 512
TN_IN = 1024
TN_MERGE = 512
TK_ACC = 512
PANEL_NCH = 512
TF = 256
TM_GU = 1024
TL = 256
LRU_CW = 512
TN_ADA = 512
POOL_HALO = 32


def _rmsnorm_rows(x, g):
    ms = jnp.mean(x * x, axis=-1, keepdims=True)
    return x * lax.rsqrt(ms + RMS_EPS) * g


def _sigmoid(x):
    return 0.5 * jnp.tanh(0.5 * x) + 0.5


def _silu(x):
    h = 0.5 * x
    return h * jnp.tanh(h) + h


def _sqrt_nonneg(v):
    return jnp.where(v > 0.0, v * lax.rsqrt(v), 0.0)


def _gelu_tanh(x):
    return 0.5 * x * (1.0 + jnp.tanh(0.7978845608028654 * (x + 0.044715 * (x * x * x))))


def _params(sem, vmem_mib):
    return pltpu.CompilerParams(dimension_semantics=sem, vmem_limit_bytes=vmem_mib * MIB)


def _ada_kernel(c_ref, w_ref, b_ref, o_ref):
    c = c_ref[...]
    ca = _silu(c).astype(BF16)
    o_ref[...] = jnp.dot(ca, w_ref[...].astype(BF16), preferred_element_type=F32) + b_ref[...]


def _ada(c_pad, w_ada, b_ada):
    d, n = w_ada.shape
    return pl.pallas_call(
        _ada_kernel,
        grid=(n // TN_ADA,),
        in_specs=[
            pl.BlockSpec((SUBLANES, d), lambda j: (0, 0)),
            pl.BlockSpec((d, TN_ADA), lambda j: (0, j)),
            pl.BlockSpec((1, TN_ADA), lambda j: (0, j)),
        ],
        out_specs=pl.BlockSpec((SUBLANES, TN_ADA), lambda j: (0, j)),
        out_shape=jax.ShapeDtypeStruct((SUBLANES, n), F32),
        compiler_params=_params(("arbitrary",), 32),
        name="ada",
    )(c_pad, w_ada, b_ada)


def _modulated_norm(x, g, mod_ref, shift_row):
    sh = mod_ref[0, shift_row:shift_row + 1, :]
    sc = mod_ref[0, shift_row + 1:shift_row + 2, :]
    return _rmsnorm_rows(x, g) * (1.0 + sc) + sh


def _inproj_kernel(x_ref, mod_ref, g_ref, w_ref, o_ref, u_ref):
    @pl.when(pl.program_id(1) == 0)
    def _():
        u_ref[...] = _modulated_norm(x_ref[...], g_ref[...], mod_ref, 0).astype(BF16)

    o_ref[...] = jnp.dot(u_ref[...], w_ref[...], preferred_element_type=F32)


def _inproj(x2d, modv, g, w, *, seq):
    t, d = x2d.shape
    n = w.shape[1]
    tiles_per_seq = seq // TM
    return pl.pallas_call(
        _inproj_kernel,
        grid=(t // TM, n // TN_IN),
        in_specs=[
            pl.BlockSpec((TM, d), lambda i, j: (i, 0)),
            pl.BlockSpec((1, SUBLANES, d), lambda i, j: (i // tiles_per_seq, 0, 0)),
            pl.BlockSpec((1, d), lambda i, j: (0, 0)),
            pl.BlockSpec((d, TN_IN), lambda i, j: (0, j)),
        ],
        out_specs=pl.BlockSpec((TM, TN_IN), lambda i, j: (i, j)),
        out_shape=jax.ShapeDtypeStruct((t, n), F32),
        scratch_shapes=[pltpu.VMEM((TM, d), BF16)],
        compiler_params=_params(("arbitrary", "arbitrary"), 56),
        name="inproj",
    )(x2d, modv, g, w)


def _gate_up_kernel(u_ref, wg_ref, wu_ref, o_ref, wb_ref, *, n_valid):
    j = pl.program_id(0)
    tf = wg_ref.shape[1]

    @pl.when(jnp.logical_and(pl.program_id(1) == 0, j < n_valid))
    def _():
        wb_ref[:, :tf] = wg_ref[...].astype(BF16)
        wb_ref[:, tf:] = wu_ref[...].astype(BF16)

    @pl.when(j < n_valid)
    def _():
        r = jnp.dot(u_ref[...], wb_ref[...], preferred_element_type=F32)
        o_ref[...] = (_silu(r[:, :tf]) * r[:, tf:]).astype(o_ref.dtype)

    @pl.when(j >= n_valid)
    def _():
        o_ref[...] = jnp.zeros(o_ref.shape, o_ref.dtype)


def _gate_up(u, w_gate_up, *, f, fp):
    t, d = u.shape
    nv = f // TF
    last = nv - 1
    return pl.pallas_call(
        functools.partial(_gate_up_kernel, n_valid=nv),
        grid=(fp // TF, t // TM_GU),
        in_specs=[
            pl.BlockSpec((TM_GU, d), lambda j, i: (i, 0)),
            pl.BlockSpec((d, TF), lambda j, i: (0, jnp.minimum(j, last))),
            pl.BlockSpec((d, TF), lambda j, i: (0, nv + jnp.minimum(j, last))),
        ],
        out_specs=pl.BlockSpec((TM_GU, TF), lambda j, i: (i, j)),
        out_shape=jax.ShapeDtypeStruct((t, fp), BF16),
        scratch_shapes=[pltpu.VMEM((d, 2 * TF), BF16)],
        compiler_params=_params(("arbitrary", "arbitrary"), 56),
        name="gate_up",
    )(u, w_gate_up, w_gate_up)


def _lru_kernel(xr_ref, gr_ref, cw_ref, cb_ref, wa_ref, ba_ref, wx_ref, bx_ref, lam_ref,
                o_ref, ext_ref, a_ref, u_ref, h_ref):
    tl, d = xr_ref.shape
    halo = SUBLANES
    t = pl.program_id(1)

    @pl.when(t == 0)
    def _():
        ext_ref[0:halo, :] = jnp.zeros((halo, d), F32)
        h_ref[...] = jnp.zeros(h_ref.shape, F32)

    @pl.when(t > 0)
    def _():
        ext_ref[0:halo, :] = ext_ref[tl:tl + halo, :]

    ext_ref[halo:, :] = xr_ref[...]

    rowid = lax.broadcasted_iota(jnp.int32, (SUBLANES, LRU_CW), 0)
    heads_per_chunk = LRU_CW // LRU_HEAD_DIM

    for c in range(d // LRU_CW):
        cs = slice(c * LRU_CW, (c + 1) * LRU_CW)
        ext = ext_ref[:, cs]
        xc = cb_ref[:, cs]
        for k in range(CONV_WIDTH):
            back = CONV_WIDTH - 1 - k
            tap = ext if back == 0 else pltpu.roll(ext, back, 0)
            xc = xc + cw_ref[k:k + 1, cs] * tap[halo:, :]
        xcb = xc.astype(BF16)
        la, lx = [], []
        for hh in range(heads_per_chunk):
            head = c * heads_per_chunk + hh
            xh = xcb[:, hh * LRU_HEAD_DIM:(hh + 1) * LRU_HEAD_DIM]
            la.append(jnp.dot(xh, wa_ref[head], preferred_element_type=F32))
            lx.append(jnp.dot(xh, wx_ref[head], preferred_element_type=F32))
        r = _sigmoid(jnp.concatenate(la, axis=1) + ba_ref[:, cs])
        i = _sigmoid(jnp.concatenate(lx, axis=1) + bx_ref[:, cs])
        z = -lam_ref[:, cs]
        softplus = jnp.maximum(z, 0.0) + jnp.log1p(jnp.exp(-jnp.abs(z)))
        log_a = (-LRU_C * softplus) * r
        a = jnp.exp(log_a)
        a_ref[...] = a
        u_ref[...] = _sqrt_nonneg(1.0 - a * a) * (i * xc)

        def blk(kk, hprev):
            r0 = pl.multiple_of(kk * SUBLANES, SUBLANES)
            a8 = a_ref[pl.ds(r0, SUBLANES), :]
            u8 = u_ref[pl.ds(r0, SUBLANES), :]
            for s in (1, 2, 4):
                a_sh = pltpu.roll(a8, s, 0)
                u_sh = pltpu.roll(u8, s, 0)
                m = rowid >= s
                u8 = jnp.where(m, a8 * u_sh + u8, u8)
                a8 = jnp.where(m, a8 * a_sh, a8)
            h8 = a8 * hprev + u8
            u_ref[pl.ds(r0, SUBLANES), :] = h8
            return h8[SUBLANES - 1:SUBLANES, :]

        h_last = lax.fori_loop(0, tl // SUBLANES, blk, h_ref[0:1, cs], unroll=4)
        h_ref[0:1, cs] = h_last
        o_ref[:, cs] = (u_ref[...] * _gelu_tanh(gr_ref[:, cs])).astype(o_ref.dtype)


def _lru(proj, conv_w, conv_b, wa, ba, wx, bx, lam, *, batch, seq, d):
    nt = seq // TL
    full2 = lambda b, t: (0, 0)
    full3 = lambda b, t: (0, 0, 0)
    nh = wa.shape[0]
    return pl.pallas_call(
        _lru_kernel,
        grid=(batch, nt),
        in_specs=[
            pl.BlockSpec((TL, d), lambda b, t: (b * nt + t, 0)),
            pl.BlockSpec((TL, d), lambda b, t: (b * nt + t, 1)),
            pl.BlockSpec((CONV_WIDTH, d), full2),
            pl.BlockSpec((1, d), full2),
            pl.BlockSpec((nh, LRU_HEAD_DIM, LRU_HEAD_DIM), full3),
            pl.BlockSpec((1, d), full2),
            pl.BlockSpec((nh, LRU_HEAD_DIM, LRU_HEAD_DIM), full3),
            pl.BlockSpec((1, d), full2),
            pl.BlockSpec((1, d), full2),
        ],
        out_specs=pl.BlockSpec((TL, d), lambda b, t: (b * nt + t, 0)),
        out_shape=jax.ShapeDtypeStruct((batch * seq, d), BF16),
        scratch_shapes=[
            pltpu.VMEM((TL + SUBLANES, d), F32),
            pltpu.VMEM((TL, LRU_CW), F32),
            pltpu.VMEM((TL, LRU_CW), F32),
            pltpu.VMEM((SUBLANES, d), F32),
        ],
        compiler_params=_params(("arbitrary", "arbitrary"), 48),
        name="lru",
    )(proj, proj, conv_w, conv_b, wa, ba, wx, bx, lam)


def _pool_kernel(xp_ref, pw_ref, ps_ref, o_ref, ext_ref, sa_ref, sb_ref):
    tl, d = xp_ref.shape
    h = POOL_HALO
    rows = tl + h
    t = pl.program_id(1)
    gd = d // len(POOL_WINDOWS)

    @pl.when(t == 0)
    def _():
        ext_ref[0:h, :] = jnp.zeros((h, d), F32)

    @pl.when(t > 0)
    def _():
        ext_ref[0:h, :] = ext_ref[tl:tl + h, :]

    ext_ref[h:, :] = xp_ref[...]

    frame = t * tl + lax.broadcasted_iota(jnp.int32, (tl, 1), 0)

    for g, w in enumerate(POOL_WINDOWS):
        cs = slice(g * gd, (g + 1) * gd)
        cur = ext_ref[8:rows, cs] + ext_ref[7:rows - 1, cs]
        start = 8
        if w >= 4:
            sa_ref[8:rows, :] = cur
            cur = sa_ref[16:rows, :] + sa_ref[14:rows - 2, :]
            start = 16
        if w >= 8:
            sb_ref[16:rows, :] = cur
            cur = sb_ref[24:rows, :] + sb_ref[20:rows - 4, :]
            start = 24
        if w >= 16:
            sa_ref[24:rows, :] = cur
            cur = sa_ref[32:rows, :] + sa_ref[24:rows - 8, :]
            start = 32
        win = cur[h - start:, :]
        cnt = jnp.minimum(frame + 1, w).astype(F32)
        pooled = (win / cnt - ext_ref[h:rows, cs]).astype(BF16)
        mixed = jnp.dot(pooled, pw_ref[g], preferred_element_type=F32) * ps_ref[:, cs]
        o_ref[:, cs] = mixed.astype(o_ref.dtype)


def _pool(proj, pool_w, pool_scale, *, batch, seq, d):
    nt = seq // TL
    ng, gd, _ = pool_w.shape
    return pl.pallas_call(
        _pool_kernel,
        grid=(batch, nt),
        in_specs=[
            pl.BlockSpec((TL, d), lambda b, t: (b * nt + t, 2)),
            pl.BlockSpec((ng, gd, gd), lambda b, t: (0, 0, 0)),
            pl.BlockSpec((1, d), lambda b, t: (0, 0)),
        ],
        out_specs=pl.BlockSpec((TL, d), lambda b, t: (b * nt + t, 0)),
        out_shape=jax.ShapeDtypeStruct((batch * seq, d), BF16),
        scratch_shapes=[
            pltpu.VMEM((TL + POOL_HALO, d), F32),
            pltpu.VMEM((TL + POOL_HALO, gd), F32),
            pltpu.VMEM((TL + POOL_HALO, gd), F32),
        ],
        compiler_params=_params(("arbitrary", "arbitrary"), 48),
        name="pool",
    )(proj, pool_w, pool_scale)


def _merge_kernel(yl_ref, yp_ref, ml_ref, mp_ref, wl_ref, wp_ref, o_ref):
    bl = jnp.dot(yl_ref[...], wl_ref[...], preferred_element_type=F32)
    bp = jnp.dot(yp_ref[...], wp_ref[...], preferred_element_type=F32)
    o_ref[...] = (_sigmoid(ml_ref[...]) * bl + _sigmoid(mp_ref[...]) * bp).astype(o_ref.dtype)


def _merge(y_lru, y_pool, proj, wl, wp):
    t, d = y_lru.shape
    tn = TN_MERGE
    col0 = (proj.shape[1] - 2 * d) // tn
    return pl.pallas_call(
        _merge_kernel,
        grid=(t // TM, d // tn),
        in_specs=[
            pl.BlockSpec((TM, d), lambda i, j: (i, 0)),
            pl.BlockSpec((TM, d), lambda i, j: (i, 0)),
            pl.BlockSpec((TM, tn), lambda i, j: (i, col0 + j)),
            pl.BlockSpec((TM, tn), lambda i, j: (i, col0 + d // tn + j)),
            pl.BlockSpec((d, tn), lambda i, j: (0, j)),
            pl.BlockSpec((d, tn), lambda i, j: (0, j)),
        ],
        out_specs=pl.BlockSpec((TM, tn), lambda i, j: (i, j)),
        out_shape=jax.ShapeDtypeStruct((t, d), BF16),
        compiler_params=_params(("arbitrary", "arbitrary"), 48),
        name="merge",
    )(y_lru, y_pool, proj, proj, wl, wp)


def _panel_kernel(a_ref, w_ref, x_ref, mod_ref, g_ref, *rest, gate_row, next_shift_row):
    if next_shift_row is None:
        (o_ref,) = rest
    else:
        gn_ref, o_ref, u_ref = rest
    k = pl.program_id(1)
    tm, d = o_ref.shape

    def accumulate(first):
        a = a_ref[...]
        for n0 in range(0, d, PANEL_NCH):
            part = jnp.dot(a, w_ref[:, n0:n0 + PANEL_NCH], preferred_element_type=F32)
            if first:
                o_ref[:, n0:n0 + PANEL_NCH] = part
            else:
                o_ref[:, n0:n0 + PANEL_NCH] += part

    pl.when(k == 0)(functools.partial(accumulate, True))
    pl.when(k > 0)(functools.partial(accumulate, False))

    @pl.when(k == pl.num_programs(1) - 1)
    def _():
        def rows(rb, carry):
            r0 = pl.multiple_of(rb * SUBLANES, SUBLANES)
            sl = pl.ds(r0, SUBLANES)
            gate = mod_ref[0, gate_row:gate_row + 1, :]
            out = x_ref[sl, :] + gate * _rmsnorm_rows(o_ref[sl, :], g_ref[...])
            o_ref[sl, :] = out
            if next_shift_row is not None:
                u_ref[sl, :] = _modulated_norm(out, gn_ref[...], mod_ref,
                                               next_shift_row).astype(BF16)
            return carry

        lax.fori_loop(0, tm // SUBLANES, rows, 0, unroll=2)


def _panel(a, w, x2d, modv, g, g_next=None, *, seq, gate_row, next_shift_row=None, name):
    t, kdim = a.shape
    d = w.shape[1]
    tiles_per_seq = seq // TM
    row_spec = pl.BlockSpec((TM, d), lambda i, k: (i, 0))
    vec_spec = pl.BlockSpec((1, d), lambda i, k: (0, 0))
    in_specs = [
        pl.BlockSpec((TM, TK_ACC), lambda i, k: (i, k)),
        pl.BlockSpec((TK_ACC, d), lambda i, k: (k, 0)),
        row_spec,
        pl.BlockSpec((1, SUBLANES, d), lambda i, k: (i // tiles_per_seq, 0, 0)),
        vec_spec,
    ]
    args = [a, w, x2d, modv, g]
    out_specs = row_spec
    out_shape = jax.ShapeDtypeStruct((t, d), F32)
    if next_shift_row is not None:
        in_specs.append(vec_spec)
        args.append(g_next)
        out_specs = [row_spec, row_spec]
        out_shape = [out_shape, jax.ShapeDtypeStruct((t, d), BF16)]
    return pl.pallas_call(
        functools.partial(_panel_kernel, gate_row=gate_row, next_shift_row=next_shift_row),
        grid=(t // TM, kdim // TK_ACC),
        in_specs=in_specs,
        out_specs=out_specs,
        out_shape=out_shape,
        compiler_params=_params(("arbitrary", "arbitrary"), 58),
        name=name,
    )(*args)


def _layer(x2d, modv, *, batch, seq, g_mix_pre, g_mix_post, w_in, conv_w, conv_b, w_rg_a, b_rg_a,
           w_rg_x, b_rg_x, lru_lambda, pool_w, pool_scale, w_branch_lru, w_branch_pool, w_o,
           g_ffn_pre, g_ffn_post, w_gate_up, w_down):
    d = x2d.shape[1]
    row = lambda v: v.reshape(1, -1)

    proj = _inproj(x2d, modv, row(g_mix_pre), w_in.astype(BF16), seq=seq)
    y_lru = _lru(proj, conv_w, row(conv_b), w_rg_a.astype(BF16), row(b_rg_a),
                 w_rg_x.astype(BF16), row(b_rg_x), row(lru_lambda), batch=batch, seq=seq, d=d)
    y_pool = _pool(proj, pool_w.astype(BF16), row(pool_scale), batch=batch, seq=seq, d=d)
    merged = _merge(y_lru, y_pool, proj, w_branch_lru.astype(BF16), w_branch_pool.astype(BF16))
    x1, u2 = _panel(merged, w_o.astype(BF16), x2d, modv, row(g_mix_post), row(g_ffn_pre), seq=seq,
                    gate_row=2, next_shift_row=3, name="wo_post")

    f = w_down.shape[0]
    assert f % TF == 0
    fp = -(-f // TK_ACC) * TK_ACC
    hid = _gate_up(u2, w_gate_up, f=f, fp=fp)
    w_dn = jnp.pad(w_down.astype(BF16), ((0, fp - f), (0, 0)))
    return _panel(hid, w_dn, x1, modv, row(g_ffn_post), seq=seq, gate_row=5, name="down_post")


def kernel(x, c, w_ada, b_ada, g_mix_pre, g_mix_post, w_in, conv_w, conv_b, w_rg_a, b_rg_a,
           w_rg_x, b_rg_x, lru_lambda, pool_w, pool_scale, w_branch_lru, w_branch_pool, w_o,
           g_ffn_pre, g_ffn_post, w_gate_up, w_down):
    batch, seq, d = x.shape
    depth = w_ada.shape[0]
    n_mod = w_ada.shape[2] // d
    assert seq % TM == 0 and seq % TL == 0 and d % TN_IN == 0 and batch <= SUBLANES

    c_pad = jnp.pad(c, ((0, SUBLANES - batch), (0, 0)))
    x2d = x.reshape(batch * seq, d)
    for l in range(depth):
        mod = _ada(c_pad, w_ada[l], b_ada[l].reshape(1, -1))
        modv = jnp.pad(mod[:batch].reshape(batch, n_mod, d), ((0, 0), (0, SUBLANES - n_mod), (0, 0)))
        x2d = _layer(
            x2d, modv, batch=batch, seq=seq,
            g_mix_pre=g_mix_pre[l], g_mix_post=g_mix_post[l], w_in=w_in[l], conv_w=conv_w[l],
            conv_b=conv_b[l], w_rg_a=w_rg_a[l], b_rg_a=b_rg_a[l], w_rg_x=w_rg_x[l],
            b_rg_x=b_rg_x[l], lru_lambda=lru_lambda[l], pool_w=pool_w[l],
            pool_scale=pool_scale[l], w_branch_lru=w_branch_lru[l], w_branch_pool=w_branch_pool[l],
            w_o=w_o[l], g_ffn_pre=g_ffn_pre[l], g_ffn_post=g_ffn_post[l],
            w_gate_up=w_gate_up[l], w_down=w_down[l])
    return x2d.reshape(batch, seq, d)
```

```python
import functools

import jax
import jax.numpy as jnp
from jax import lax
from jax.experimental import pallas as pl
from jax.experimental.pallas import tpu as pltpu

F32 = jnp.float32
BF16 = jnp.bfloat16

RMS_EPS = 1e-6
LRU_C = 8.0
LRU_HEAD_DIM = 256
CONV_WIDTH = 4
POOL_WINDOWS = (2, 4, 8, 16)

SUBLANES = 8
MIB = 1024 * 1024

TM = 512
TN_IN = 1024
TN_MERGE = 512
TK_ACC = 512
PANEL_NCH = 512
PANEL_ROWS = 64
TF = 256
TM_GU = 1024
TL = 256
LRU_CW = 512
TN_ADA = 512
POOL_HALO = 32


def _rmsnorm_rows(x, g):
    ms = jnp.mean(x * x, axis=-1, keepdims=True)
    return x * lax.rsqrt(ms + RMS_EPS) * g


def _sigmoid(x):
    return 0.5 * jnp.tanh(0.5 * x) + 0.5


def _silu(x):
    h = 0.5 * x
    return h * jnp.tanh(h) + h


def _sqrt_nonneg(v):
    return jnp.where(v > 0.0, v * lax.rsqrt(v), 0.0)


def _gelu_tanh(x):
    return 0.5 * x * (1.0 + jnp.tanh(0.7978845608028654 * (x + 0.044715 * (x * x * x))))


def _params(sem, vmem_mib):
    return pltpu.CompilerParams(dimension_semantics=sem, vmem_limit_bytes=vmem_mib * MIB)


def _ada_kernel(c_ref, w_ref, b_ref, o_ref):
    c = c_ref[...]
    ca = _silu(c).astype(BF16)
    o_ref[...] = jnp.dot(ca, w_ref[...].astype(BF16), preferred_element_type=F32) + b_ref[...]


def _ada(c_pad, w_ada, b_ada):
    d, n = w_ada.shape
    return pl.pallas_call(
        _ada_kernel,
        grid=(n // TN_ADA,),
        in_specs=[
            pl.BlockSpec((SUBLANES, d), lambda j: (0, 0)),
            pl.BlockSpec((d, TN_ADA), lambda j: (0, j)),
            pl.BlockSpec((1, TN_ADA), lambda j: (0, j)),
        ],
        out_specs=pl.BlockSpec((SUBLANES, TN_ADA), lambda j: (0, j)),
        out_shape=jax.ShapeDtypeStruct((SUBLANES, n), F32),
        compiler_params=_params(("arbitrary",), 32),
        name="ada",
    )(c_pad, w_ada, b_ada)


def _modulated_norm(x, g, mod_ref, shift_row):
    sh = mod_ref[0, shift_row:shift_row + 1, :]
    sc = mod_ref[0, shift_row + 1:shift_row + 2, :]
    return _rmsnorm_rows(x, g) * (1.0 + sc) + sh


def _inproj_kernel(x_ref, mod_ref, g_ref, w_ref, o_ref, u_ref):
    @pl.when(pl.program_id(1) == 0)
    def _():
        u_ref[...] = _modulated_norm(x_ref[...], g_ref[...], mod_ref, 0).astype(BF16)

    o_ref[...] = jnp.dot(u_ref[...], w_ref[...], preferred_element_type=F32)


def _inproj(x2d, modv, g, w, *, seq):
    t, d = x2d.shape
    n = w.shape[1]
    tiles_per_seq = seq // TM
    return pl.pallas_call(
        _inproj_kernel,
        grid=(t // TM, n // TN_IN),
        in_specs=[
            pl.BlockSpec((TM, d), lambda i, j: (i, 0)),
            pl.BlockSpec((1, SUBLANES, d), lambda i, j: (i // tiles_per_seq, 0, 0)),
            pl.BlockSpec((1, d), lambda i, j: (0, 0)),
            pl.BlockSpec((d, TN_IN), lambda i, j: (0, j)),
        ],
        out_specs=pl.BlockSpec((TM, TN_IN), lambda i, j: (i, j)),
        out_shape=jax.ShapeDtypeStruct((t, n), F32),
        scratch_shapes=[pltpu.VMEM((TM, d), BF16)],
        compiler_params=_params(("arbitrary", "arbitrary"), 56),
        name="inproj",
    )(x2d, modv, g, w)


def _gate_up_kernel(u_ref, wg_ref, wu_ref, o_ref, wb_ref, *, n_valid):
    j = pl.program_id(0)
    tf = wg_ref.shape[1]

    @pl.when(jnp.logical_and(pl.program_id(1) == 0, j < n_valid))
    def _():
        wb_ref[:, :tf] = wg_ref[...].astype(BF16)
        wb_ref[:, tf:] = wu_ref[...].astype(BF16)

    @pl.when(j < n_valid)
    def _():
        r = jnp.dot(u_ref[...], wb_ref[...], preferred_element_type=F32)
        o_ref[...] = (_silu(r[:, :tf]) * r[:, tf:]).astype(o_ref.dtype)

    @pl.when(j >= n_valid)
    def _():
        o_ref[...] = jnp.zeros(o_ref.shape, o_ref.dtype)


def _gate_up(u, w_gate_up, *, f, fp):
    t, d = u.shape
    nv = f // TF
    last = nv - 1
    return pl.pallas_call(
        functools.partial(_gate_up_kernel, n_valid=nv),
        grid=(fp // TF, t // TM_GU),
        in_specs=[
            pl.BlockSpec((TM_GU, d), lambda j, i: (i, 0)),
            pl.BlockSpec((d, TF), lambda j, i: (0, jnp.minimum(j, last))),
            pl.BlockSpec((d, TF), lambda j, i: (0, nv + jnp.minimum(j, last))),
        ],
        out_specs=pl.BlockSpec((TM_GU, TF), lambda j, i: (i, j)),
        out_shape=jax.ShapeDtypeStruct((t, fp), BF16),
        scratch_shapes=[pltpu.VMEM((d, 2 * TF), BF16)],
        compiler_params=_params(("arbitrary", "arbitrary"), 56),
        name="gate_up",
    )(u, w_gate_up, w_gate_up)


def _lru_kernel(xr_ref, gr_ref, cw_ref, cb_ref, wa_ref, ba_ref, wx_ref, bx_ref, lam_ref,
                o_ref, ext_ref, a_ref, u_ref, h_ref):
    tl, d = xr_ref.shape
    halo = SUBLANES
    t = pl.program_id(1)

    @pl.when(t == 0)
    def _():
        ext_ref[0:halo, :] = jnp.zeros((halo, d), F32)
        h_ref[...] = jnp.zeros(h_ref.shape, F32)

    @pl.when(t > 0)
    def _():
        ext_ref[0:halo, :] = ext_ref[tl:tl + halo, :]

    ext_ref[halo:, :] = xr_ref[...]

    rowid = lax.broadcasted_iota(jnp.int32, (SUBLANES, LRU_CW), 0)
    heads_per_chunk = LRU_CW // LRU_HEAD_DIM

    for c in range(d // LRU_CW):
        cs = slice(c * LRU_CW, (c + 1) * LRU_CW)
        ext = ext_ref[:, cs]
        xc = cb_ref[:, cs]
        for k in range(CONV_WIDTH):
            back = CONV_WIDTH - 1 - k
            tap = ext if back == 0 else pltpu.roll(ext, back, 0)
            xc = xc + cw_ref[k:k + 1, cs] * tap[halo:, :]
        xcb = xc.astype(BF16)
        la, lx = [], []
        for hh in range(heads_per_chunk):
            head = c * heads_per_chunk + hh
            xh = xcb[:, hh * LRU_HEAD_DIM:(hh + 1) * LRU_HEAD_DIM]
            la.append(jnp.dot(xh, wa_ref[head], preferred_element_type=F32))
            lx.append(jnp.dot(xh, wx_ref[head], preferred_element_type=F32))
        r = _sigmoid(jnp.concatenate(la, axis=1) + ba_ref[:, cs])
        i = _sigmoid(jnp.concatenate(lx, axis=1) + bx_ref[:, cs])
        z = -lam_ref[:, cs]
        softplus = jnp.maximum(z, 0.0) + jnp.log1p(jnp.exp(-jnp.abs(z)))
        log_a = (-LRU_C * softplus) * r
        a = jnp.exp(log_a)
        a_ref[...] = a
        u_ref[...] = _sqrt_nonneg(1.0 - a * a) * (i * xc)

        def blk(kk, hprev):
            r0 = pl.multiple_of(kk * SUBLANES, SUBLANES)
            a8 = a_ref[pl.ds(r0, SUBLANES), :]
            u8 = u_ref[pl.ds(r0, SUBLANES), :]
            for s in (1, 2, 4):
                a_sh = pltpu.roll(a8, s, 0)
                u_sh = pltpu.roll(u8, s, 0)
                m = rowid >= s
                u8 = jnp.where(m, a8 * u_sh + u8, u8)
                a8 = jnp.where(m, a8 * a_sh, a8)
            h8 = a8 * hprev + u8
            u_ref[pl.ds(r0, SUBLANES), :] = h8
            return h8[SUBLANES - 1:SUBLANES, :]

        h_last = lax.fori_loop(0, tl // SUBLANES, blk, h_ref[0:1, cs], unroll=4)
        h_ref[0:1, cs] = h_last
        o_ref[:, cs] = (u_ref[...] * _gelu_tanh(gr_ref[:, cs])).astype(o_ref.dtype)


def _lru(proj, conv_w, conv_b, wa, ba, wx, bx, lam, *, batch, seq, d):
    nt = seq // TL
    full2 = lambda b, t: (0, 0)
    full3 = lambda b, t: (0, 0, 0)
    nh = wa.shape[0]
    return pl.pallas_call(
        _lru_kernel,
        grid=(batch, nt),
        in_specs=[
            pl.BlockSpec((TL, d), lambda b, t: (b * nt + t, 0)),
            pl.BlockSpec((TL, d), lambda b, t: (b * nt + t, 1)),
            pl.BlockSpec((CONV_WIDTH, d), full2),
            pl.BlockSpec((1, d), full2),
            pl.BlockSpec((nh, LRU_HEAD_DIM, LRU_HEAD_DIM), full3),
            pl.BlockSpec((1, d), full2),
            pl.BlockSpec((nh, LRU_HEAD_DIM, LRU_HEAD_DIM), full3),
            pl.BlockSpec((1, d), full2),
            pl.BlockSpec((1, d), full2),
        ],
        out_specs=pl.BlockSpec((TL, d), lambda b, t: (b * nt + t, 0)),
        out_shape=jax.ShapeDtypeStruct((batch * seq, d), BF16),
        scratch_shapes=[
            pltpu.VMEM((TL + SUBLANES, d), F32),
            pltpu.VMEM((TL, LRU_CW), F32),
            pltpu.VMEM((TL, LRU_CW), F32),
            pltpu.VMEM((SUBLANES, d), F32),
        ],
        compiler_params=_params(("arbitrary", "arbitrary"), 48),
        name="lru",
    )(proj, proj, conv_w, conv_b, wa, ba, wx, bx, lam)


def _pool_kernel(xp_ref, pw_ref, ps_ref, o_ref, ext_ref, sa_ref, sb_ref):
    tl, d = xp_ref.shape
    h = POOL_HALO
    rows = tl + h
    t = pl.program_id(1)
    gd = d // len(POOL_WINDOWS)

    @pl.when(t == 0)
    def _():
        ext_ref[0:h, :] = jnp.zeros((h, d), F32)

    @pl.when(t > 0)
    def _():
        ext_ref[0:h, :] = ext_ref[tl:tl + h, :]

    ext_ref[h:, :] = xp_ref[...]

    frame = t * tl + lax.broadcasted_iota(jnp.int32, (tl, 1), 0)

    for g, w in enumerate(POOL_WINDOWS):
        cs = slice(g * gd, (g + 1) * gd)
        cur = ext_ref[8:rows, cs] + ext_ref[7:rows - 1, cs]
        start = 8
        if w >= 4:
            sa_ref[8:rows, :] = cur
            cur = sa_ref[16:rows, :] + sa_ref[14:rows - 2, :]
            start = 16
        if w >= 8:
            sb_ref[16:rows, :] = cur
            cur = sb_ref[24:rows, :] + sb_ref[20:rows - 4, :]
            start = 24
        if w >= 16:
            sa_ref[24:rows, :] = cur
            cur = sa_ref[32:rows, :] + sa_ref[24:rows - 8, :]
            start = 32
        win = cur[h - start:, :]
        cnt = jnp.minimum(frame + 1, w).astype(F32)
        pooled = (win / cnt - ext_ref[h:rows, cs]).astype(BF16)
        mixed = jnp.dot(pooled, pw_ref[g], preferred_element_type=F32) * ps_ref[:, cs]
        o_ref[:, cs] = mixed.astype(o_ref.dtype)


def _pool(proj, pool_w, pool_scale, *, batch, seq, d):
    nt = seq // TL
    ng, gd, _ = pool_w.shape
    return pl.pallas_call(
        _pool_kernel,
        grid=(batch, nt),
        in_specs=[
            pl.BlockSpec((TL, d), lambda b, t: (b * nt + t, 2)),
            pl.BlockSpec((ng, gd, gd), lambda b, t: (0, 0, 0)),
            pl.BlockSpec((1, d), lambda b, t: (0, 0)),
        ],
        out_specs=pl.BlockSpec((TL, d), lambda b, t: (b * nt + t, 0)),
        out_shape=jax.ShapeDtypeStruct((batch * seq, d), BF16),
        scratch_shapes=[
            pltpu.VMEM((TL + POOL_HALO, d), F32),
            pltpu.VMEM((TL + POOL_HALO, gd), F32),
            pltpu.VMEM((TL + POOL_HALO, gd), F32),
        ],
        compiler_params=_params(("arbitrary", "arbitrary"), 48),
        name="pool",
    )(proj, pool_w, pool_scale)


def _merge_kernel(yl_ref, yp_ref, ml_ref, mp_ref, wl_ref, wp_ref, o_ref):
    bl = jnp.dot(yl_ref[...], wl_ref[...], preferred_element_type=F32)
    bp = jnp.dot(yp_ref[...], wp_ref[...], preferred_element_type=F32)
    o_ref[...] = (_sigmoid(ml_ref[...]) * bl + _sigmoid(mp_ref[...]) * bp).astype(o_ref.dtype)


def _merge(y_lru, y_pool, proj, wl, wp):
    t, d = y_lru.shape
    tn = TN_MERGE
    col0 = (proj.shape[1] - 2 * d) // tn
    return pl.pallas_call(
        _merge_kernel,
        grid=(t // TM, d // tn),
        in_specs=[
            pl.BlockSpec((TM, d), lambda i, j: (i, 0)),
            pl.BlockSpec((TM, d), lambda i, j: (i, 0)),
            pl.BlockSpec((TM, tn), lambda i, j: (i, col0 + j)),
            pl.BlockSpec((TM, tn), lambda i, j: (i, col0 + d // tn + j)),
            pl.BlockSpec((d, tn), lambda i, j: (0, j)),
            pl.BlockSpec((d, tn), lambda i, j: (0, j)),
        ],
        out_specs=pl.BlockSpec((TM, tn), lambda i, j: (i, j)),
        out_shape=jax.ShapeDtypeStruct((t, d), BF16),
        compiler_params=_params(("arbitrary", "arbitrary"), 48),
        name="merge",
    )(y_lru, y_pool, proj, proj, wl, wp)


def _panel_kernel(a_ref, w_ref, x_ref, mod_ref, g_ref, *rest, gate_row, next_shift_row):
    if next_shift_row is None:
        o_ref, vec_ref = rest
    else:
        gn_ref, o_ref, u_ref, vec_ref = rest
    k = pl.program_id(1)
    tm, d = o_ref.shape

    def accumulate(first):
        a = a_ref[...]
        for n0 in range(0, d, PANEL_NCH):
            part = jnp.dot(a, w_ref[:, n0:n0 + PANEL_NCH], preferred_element_type=F32)
            if first:
                o_ref[:, n0:n0 + PANEL_NCH] = part
            else:
                o_ref[:, n0:n0 + PANEL_NCH] += part

    pl.when(k == 0)(functools.partial(accumulate, True))
    pl.when(k > 0)(functools.partial(accumulate, False))

    @pl.when(k == pl.num_programs(1) - 1)
    def _():
        vec_ref[0:1, :] = mod_ref[0, gate_row:gate_row + 1, :] * g_ref[...]
        if next_shift_row is not None:
            vec_ref[1:2, :] = gn_ref[...] * (1.0 + mod_ref[0, next_shift_row + 1:next_shift_row + 2, :])

        def inv_rms(v):
            return lax.rsqrt(jnp.mean(v * v, axis=-1, keepdims=True) + RMS_EPS)

        def rows(rb, carry):
            sl = pl.ds(pl.multiple_of(rb * PANEL_ROWS, PANEL_ROWS), PANEL_ROWS)
            inv = inv_rms(o_ref[sl, :])
            o_ref[sl, :] = x_ref[sl, :] + vec_ref[0:1, :] * (o_ref[sl, :] * inv)
            if next_shift_row is not None:
                inv2 = inv_rms(o_ref[sl, :])
                shift = mod_ref[0, next_shift_row:next_shift_row + 1, :]
                u_ref[sl, :] = (vec_ref[1:2, :] * (o_ref[sl, :] * inv2) + shift).astype(BF16)
            return carry

        lax.fori_loop(0, tm // PANEL_ROWS, rows, 0)


def _panel(a, w, x2d, modv, g, g_next=None, *, seq, gate_row, next_shift_row=None, name):
    t, kdim = a.shape
    d = w.shape[1]
    tiles_per_seq = seq // TM
    row_spec = pl.BlockSpec((TM, d), lambda i, k: (i, 0))
    vec_spec = pl.BlockSpec((1, d), lambda i, k: (0, 0))
    in_specs = [
        pl.BlockSpec((TM, TK_ACC), lambda i, k: (i, k)),
        pl.BlockSpec((TK_ACC, d), lambda i, k: (k, 0)),
        row_spec,
        pl.BlockSpec((1, SUBLANES, d), lambda i, k: (i // tiles_per_seq, 0, 0)),
        vec_spec,
    ]
    args = [a, w, x2d, modv, g]
    out_specs = row_spec
    out_shape = jax.ShapeDtypeStruct((t, d), F32)
    if next_shift_row is not None:
        in_specs.append(vec_spec)
        args.append(g_next)
        out_specs = [row_spec, row_spec]
        out_shape = [out_shape, jax.ShapeDtypeStruct((t, d), BF16)]
    return pl.pallas_call(
        functools.partial(_panel_kernel, gate_row=gate_row, next_shift_row=next_shift_row),
        grid=(t // TM, kdim // TK_ACC),
        in_specs=in_specs,
        out_specs=out_specs,
        out_shape=out_shape,
        scratch_shapes=[pltpu.VMEM((SUBLANES, d), F32)],
        compiler_params=_params(("arbitrary", "arbitrary"), 58),
        name=name,
    )(*args)


def _layer(x2d, modv, *, batch, seq, g_mix_pre, g_mix_post, w_in, conv_w, conv_b, w_rg_a, b_rg_a,
           w_rg_x, b_rg_x, lru_lambda, pool_w, pool_scale, w_branch_lru, w_branch_pool, w_o,
           g_ffn_pre, g_ffn_post, w_gate_up, w_down):
    d = x2d.shape[1]
    row = lambda v: v.reshape(1, -1)

    proj = _inproj(x2d, modv, row(g_mix_pre), w_in.astype(BF16), seq=seq)
    y_lru = _lru(proj, conv_w, row(conv_b), w_rg_a.astype(BF16), row(b_rg_a),
                 w_rg_x.astype(BF16), row(b_rg_x), row(lru_lambda), batch=batch, seq=seq, d=d)
    y_pool = _pool(proj, pool_w.astype(BF16), row(pool_scale), batch=batch, seq=seq, d=d)
    merged = _merge(y_lru, y_pool, proj, w_branch_lru.astype(BF16), w_branch_pool.astype(BF16))
    x1, u2 = _panel(merged, w_o.astype(BF16), x2d, modv, row(g_mix_post), row(g_ffn_pre), seq=seq,
                    gate_row=2, next_shift_row=3, name="wo_post")

    f = w_down.shape[0]
    assert f % TF == 0
    fp = -(-f // TK_ACC) * TK_ACC
    hid = _gate_up(u2, w_gate_up, f=f, fp=fp)
    w_dn = jnp.pad(w_down.astype(BF16), ((0, fp - f), (0, 0)))
    return _panel(hid, w_dn, x1, modv, row(g_ffn_post), seq=seq, gate_row=5, name="down_post")


def kernel(x, c, w_ada, b_ada, g_mix_pre, g_mix_post, w_in, conv_w, conv_b, w_rg_a, b_rg_a,
           w_rg_x, b_rg_x, lru_lambda, pool_w, pool_scale, w_branch_lru, w_branch_pool, w_o,
           g_ffn_pre, g_ffn_post, w_gate_up, w_down):
    batch, seq, d = x.shape
    depth = w_ada.shape[0]
    n_mod = w_ada.shape[2] // d
    assert seq % TM == 0 and seq % TL == 0 and d % TN_IN == 0 and batch <= SUBLANES

    c_pad = jnp.pad(c, ((0, SUBLANES - batch), (0, 0)))
    x2d = x.reshape(batch * seq, d)
    for l in range(depth):
        mod = _ada(c_pad, w_ada[l], b_ada[l].reshape(1, -1))
        modv = jnp.pad(mod[:batch].reshape(batch, n_mod, d), ((0, 0), (0, SUBLANES - n_mod), (0, 0)))
        x2d = _layer(
            x2d, modv, batch=batch, seq=seq,
            g_mix_pre=g_mix_pre[l], g_mix_post=g_mix_post[l], w_in=w_in[l], conv_w=conv_w[l],
            conv_b=conv_b[l], w_rg_a=w_rg_a[l], b_rg_a=b_rg_a[l], w_rg_x=w_rg_x[l],
            b_rg_x=b_rg_x[l], lru_lambda=lru_lambda[l], pool_w=pool_w[l],
            pool_scale=pool_scale[l], w_branch_lru=w_branch_lru[l], w_branch_pool=w_branch_pool[l],
            w_o=w_o[l], g_ffn_pre=g_ffn_pre[l], g_ffn_post=g_ffn_post[l],
            w_gate_up=w_gate_up[l], w_down=w_down[l])
    return x2d.reshape(batch, seq, d)
```

```python
import functools

import jax
import jax.numpy as jnp
from jax import lax
from jax.experimental import pallas as pl
from jax.experimental.pallas import tpu as pltpu

F32 = jnp.float32
BF16 = jnp.bfloat16

RMS_EPS = 1e-6
LRU_C = 8.0
LRU_HEAD_DIM = 256
CONV_WIDTH = 4
POOL_WINDOWS = (2, 4, 8, 16)

SUBLANES = 8
MIB = 1024 * 1024

TM = 512
TM_IN = 1024
TN_IN = 1024
TN_MERGE = 512
TK_WO = 512
TK_DOWN = 1024
PANEL_NCH = 512
PANEL_ROWS = 64
TF = 256
TM_GU = 1024
TL = 256
LRU_CW = 512
TN_ADA = 512
POOL_HALO = 32


def _rmsnorm_rows(x, g):
    ms = jnp.mean(x * x, axis=-1, keepdims=True)
    return x * lax.rsqrt(ms + RMS_EPS) * g


def _sigmoid(x):
    return 0.5 * jnp.tanh(0.5 * x) + 0.5


def _silu(x):
    h = 0.5 * x
    return h * jnp.tanh(h) + h


def _sqrt_nonneg(v):
    return jnp.where(v > 0.0, v * lax.rsqrt(v), 0.0)


def _gelu_tanh(x):
    return 0.5 * x * (1.0 + jnp.tanh(0.7978845608028654 * (x + 0.044715 * (x * x * x))))


def _params(sem, vmem_mib):
    return pltpu.CompilerParams(dimension_semantics=sem, vmem_limit_bytes=vmem_mib * MIB)


def _ada_kernel(c_ref, w_ref, b_ref, o_ref):
    c = c_ref[...]
    ca = _silu(c).astype(BF16)
    o_ref[...] = jnp.dot(ca, w_ref[...].astype(BF16), preferred_element_type=F32) + b_ref[...]


def _ada(c_pad, w_ada, b_ada):
    d, n = w_ada.shape
    return pl.pallas_call(
        _ada_kernel,
        grid=(n // TN_ADA,),
        in_specs=[
            pl.BlockSpec((SUBLANES, d), lambda j: (0, 0)),
            pl.BlockSpec((d, TN_ADA), lambda j: (0, j)),
            pl.BlockSpec((1, TN_ADA), lambda j: (0, j)),
        ],
        out_specs=pl.BlockSpec((SUBLANES, TN_ADA), lambda j: (0, j)),
        out_shape=jax.ShapeDtypeStruct((SUBLANES, n), F32),
        compiler_params=_params(("arbitrary",), 32),
        name="ada",
    )(c_pad, w_ada, b_ada)


def _modulated_norm(x, g, mod_ref, shift_row):
    sh = mod_ref[0, shift_row:shift_row + 1, :]
    sc = mod_ref[0, shift_row + 1:shift_row + 2, :]
    return _rmsnorm_rows(x, g) * (1.0 + sc) + sh


def _prenorm_kernel(x_ref, mod_ref, g_ref, o_ref):
    o_ref[...] = _modulated_norm(x_ref[...], g_ref[...], mod_ref, 0).astype(o_ref.dtype)


def _prenorm(x2d, modv, g, *, seq):
    t, d = x2d.shape
    tiles_per_seq = seq // TL
    return pl.pallas_call(
        _prenorm_kernel,
        grid=(t // TL,),
        in_specs=[
            pl.BlockSpec((TL, d), lambda i: (i, 0)),
            pl.BlockSpec((1, SUBLANES, d), lambda i: (i // tiles_per_seq, 0, 0)),
            pl.BlockSpec((1, d), lambda i: (0, 0)),
        ],
        out_specs=pl.BlockSpec((TL, d), lambda i: (i, 0)),
        out_shape=jax.ShapeDtypeStruct((t, d), BF16),
        compiler_params=_params(("arbitrary",), 40),
        name="prenorm",
    )(x2d, modv, g)


def _matmul_kernel(a_ref, w_ref, o_ref):
    o_ref[...] = jnp.dot(a_ref[...], w_ref[...], preferred_element_type=F32)


def _inproj(u, w):
    t, d = u.shape
    n = w.shape[1]
    return pl.pallas_call(
        _matmul_kernel,
        grid=(t // TM_IN, n // TN_IN),
        in_specs=[
            pl.BlockSpec((TM_IN, d), lambda i, j: (i, 0)),
            pl.BlockSpec((d, TN_IN), lambda i, j: (0, j)),
        ],
        out_specs=pl.BlockSpec((TM_IN, TN_IN), lambda i, j: (i, j)),
        out_shape=jax.ShapeDtypeStruct((t, n), F32),
        compiler_params=_params(("arbitrary", "arbitrary"), 56),
        name="inproj",
    )(u, w)


def _gate_up_kernel(u_ref, wg_ref, wu_ref, o_ref, wb_ref, *, n_valid):
    j = pl.program_id(0)
    tf = wg_ref.shape[1]

    @pl.when(jnp.logical_and(pl.program_id(1) == 0, j < n_valid))
    def _():
        wb_ref[:, :tf] = wg_ref[...].astype(BF16)
        wb_ref[:, tf:] = wu_ref[...].astype(BF16)

    @pl.when(j < n_valid)
    def _():
        r = jnp.dot(u_ref[...], wb_ref[...], preferred_element_type=F32)
        o_ref[...] = (_silu(r[:, :tf]) * r[:, tf:]).astype(o_ref.dtype)

    @pl.when(j >= n_valid)
    def _():
        o_ref[...] = jnp.zeros(o_ref.shape, o_ref.dtype)


def _gate_up(u, w_gate_up, *, f, fp):
    t, d = u.shape
    nv = f // TF
    last = nv - 1
    return pl.pallas_call(
        functools.partial(_gate_up_kernel, n_valid=nv),
        grid=(fp // TF, t // TM_GU),
        in_specs=[
            pl.BlockSpec((TM_GU, d), lambda j, i: (i, 0)),
            pl.BlockSpec((d, TF), lambda j, i: (0, jnp.minimum(j, last))),
            pl.BlockSpec((d, TF), lambda j, i: (0, nv + jnp.minimum(j, last))),
        ],
        out_specs=pl.BlockSpec((TM_GU, TF), lambda j, i: (i, j)),
        out_shape=jax.ShapeDtypeStruct((t, fp), BF16),
        scratch_shapes=[pltpu.VMEM((d, 2 * TF), BF16)],
        compiler_params=_params(("arbitrary", "arbitrary"), 56),
        name="gate_up",
    )(u, w_gate_up, w_gate_up)


def _lru_kernel(xr_ref, gr_ref, cw_ref, cb_ref, wa_ref, ba_ref, wx_ref, bx_ref, lam_ref,
                o_ref, ext_ref, a_ref, u_ref, h_ref):
    tl, d = xr_ref.shape
    halo = SUBLANES
    t = pl.program_id(1)

    @pl.when(t == 0)
    def _():
        ext_ref[0:halo, :] = jnp.zeros((halo, d), F32)
        h_ref[...] = jnp.zeros(h_ref.shape, F32)

    @pl.when(t > 0)
    def _():
        ext_ref[0:halo, :] = ext_ref[tl:tl + halo, :]

    ext_ref[halo:, :] = xr_ref[...]

    rowid = lax.broadcasted_iota(jnp.int32, (SUBLANES, LRU_CW), 0)
    heads_per_chunk = LRU_CW // LRU_HEAD_DIM

    for c in range(d // LRU_CW):
        cs = slice(c * LRU_CW, (c + 1) * LRU_CW)
        ext = ext_ref[:, cs]
        xc = cb_ref[:, cs]
        for k in range(CONV_WIDTH):
            back = CONV_WIDTH - 1 - k
            tap = ext if back == 0 else pltpu.roll(ext, back, 0)
            xc = xc + cw_ref[k:k + 1, cs] * tap[halo:, :]
        xcb = xc.astype(BF16)
        la, lx = [], []
        for hh in range(heads_per_chunk):
            head = c * heads_per_chunk + hh
            xh = xcb[:, hh * LRU_HEAD_DIM:(hh + 1) * LRU_HEAD_DIM]
            la.append(jnp.dot(xh, wa_ref[head], preferred_element_type=F32))
            lx.append(jnp.dot(xh, wx_ref[head], preferred_element_type=F32))
        r = _sigmoid(jnp.concatenate(la, axis=1) + ba_ref[:, cs])
        i = _sigmoid(jnp.concatenate(lx, axis=1) + bx_ref[:, cs])
        z = -lam_ref[:, cs]
        softplus = jnp.maximum(z, 0.0) + jnp.log1p(jnp.exp(-jnp.abs(z)))
        log_a = (-LRU_C * softplus) * r
        a = jnp.exp(log_a)
        a_ref[...] = a
        u_ref[...] = _sqrt_nonneg(1.0 - a * a) * (i * xc)

        def blk(kk, hprev):
            r0 = pl.multiple_of(kk * SUBLANES, SUBLANES)
            a8 = a_ref[pl.ds(r0, SUBLANES), :]
            u8 = u_ref[pl.ds(r0, SUBLANES), :]
            for s in (1, 2, 4):
                a_sh = pltpu.roll(a8, s, 0)
                u_sh = pltpu.roll(u8, s, 0)
                m = rowid >= s
                u8 = jnp.where(m, a8 * u_sh + u8, u8)
                a8 = jnp.where(m, a8 * a_sh, a8)
            h8 = a8 * hprev + u8
            u_ref[pl.ds(r0, SUBLANES), :] = h8
            return h8[SUBLANES - 1:SUBLANES, :]

        h_last = lax.fori_loop(0, tl // SUBLANES, blk, h_ref[0:1, cs], unroll=4)
        h_ref[0:1, cs] = h_last
        o_ref[:, cs] = (u_ref[...] * _gelu_tanh(gr_ref[:, cs])).astype(o_ref.dtype)


def _lru(proj, conv_w, conv_b, wa, ba, wx, bx, lam, *, batch, seq, d):
    nt = seq // TL
    full2 = lambda b, t: (0, 0)
    full3 = lambda b, t: (0, 0, 0)
    nh = wa.shape[0]
    return pl.pallas_call(
        _lru_kernel,
        grid=(batch, nt),
        in_specs=[
            pl.BlockSpec((TL, d), lambda b, t: (b * nt + t, 0)),
            pl.BlockSpec((TL, d), lambda b, t: (b * nt + t, 1)),
            pl.BlockSpec((CONV_WIDTH, d), full2),
            pl.BlockSpec((1, d), full2),
            pl.BlockSpec((nh, LRU_HEAD_DIM, LRU_HEAD_DIM), full3),
            pl.BlockSpec((1, d), full2),
            pl.BlockSpec((nh, LRU_HEAD_DIM, LRU_HEAD_DIM), full3),
            pl.BlockSpec((1, d), full2),
            pl.BlockSpec((1, d), full2),
        ],
        out_specs=pl.BlockSpec((TL, d), lambda b, t: (b * nt + t, 0)),
        out_shape=jax.ShapeDtypeStruct((batch * seq, d), BF16),
        scratch_shapes=[
            pltpu.VMEM((TL + SUBLANES, d), F32),
            pltpu.VMEM((TL, LRU_CW), F32),
            pltpu.VMEM((TL, LRU_CW), F32),
            pltpu.VMEM((SUBLANES, d), F32),
        ],
        compiler_params=_params(("arbitrary", "arbitrary"), 48),
        name="lru",
    )(proj, proj, conv_w, conv_b, wa, ba, wx, bx, lam)


def _pool_kernel(xp_ref, pw_ref, ps_ref, o_ref, ext_ref, sa_ref, sb_ref):
    tl, d = xp_ref.shape
    h = POOL_HALO
    rows = tl + h
    t = pl.program_id(1)
    gd = d // len(POOL_WINDOWS)

    @pl.when(t == 0)
    def _():
        ext_ref[0:h, :] = jnp.zeros((h, d), F32)

    @pl.when(t > 0)
    def _():
        ext_ref[0:h, :] = ext_ref[tl:tl + h, :]

    ext_ref[h:, :] = xp_ref[...]

    frame = t * tl + lax.broadcasted_iota(jnp.int32, (tl, 1), 0)

    for g, w in enumerate(POOL_WINDOWS):
        cs = slice(g * gd, (g + 1) * gd)
        cur = ext_ref[8:rows, cs] + ext_ref[7:rows - 1, cs]
        start = 8
        if w >= 4:
            sa_ref[8:rows, :] = cur
            cur = sa_ref[16:rows, :] + sa_ref[14:rows - 2, :]
            start = 16
        if w >= 8:
            sb_ref[16:rows, :] = cur
            cur = sb_ref[24:rows, :] + sb_ref[20:rows - 4, :]
            start = 24
        if w >= 16:
            sa_ref[24:rows, :] = cur
            cur = sa_ref[32:rows, :] + sa_ref[24:rows - 8, :]
            start = 32
        win = cur[h - start:, :]
        cnt = jnp.minimum(frame + 1, w).astype(F32)
        pooled = (win / cnt - ext_ref[h:rows, cs]).astype(BF16)
        mixed = jnp.dot(pooled, pw_ref[g], preferred_element_type=F32) * ps_ref[:, cs]
        o_ref[:, cs] = mixed.astype(o_ref.dtype)


def _pool(proj, pool_w, pool_scale, *, batch, seq, d):
    nt = seq // TL
    ng, gd, _ = pool_w.shape
    return pl.pallas_call(
        _pool_kernel,
        grid=(batch, nt),
        in_specs=[
            pl.BlockSpec((TL, d), lambda b, t: (b * nt + t, 2)),
            pl.BlockSpec((ng, gd, gd), lambda b, t: (0, 0, 0)),
            pl.BlockSpec((1, d), lambda b, t: (0, 0)),
        ],
        out_specs=pl.BlockSpec((TL, d), lambda b, t: (b * nt + t, 0)),
        out_shape=jax.ShapeDtypeStruct((batch * seq, d), BF16),
        scratch_shapes=[
            pltpu.VMEM((TL + POOL_HALO, d), F32),
            pltpu.VMEM((TL + POOL_HALO, gd), F32),
            pltpu.VMEM((TL + POOL_HALO, gd), F32),
        ],
        compiler_params=_params(("arbitrary", "arbitrary"), 48),
        name="pool",
    )(proj, pool_w, pool_scale)


def _merge_kernel(yl_ref, yp_ref, ml_ref, mp_ref, wl_ref, wp_ref, o_ref):
    bl = jnp.dot(yl_ref[...], wl_ref[...], preferred_element_type=F32)
    bp = jnp.dot(yp_ref[...], wp_ref[...], preferred_element_type=F32)
    o_ref[...] = (_sigmoid(ml_ref[...]) * bl + _sigmoid(mp_ref[...]) * bp).astype(o_ref.dtype)


def _merge(y_lru, y_pool, proj, wl, wp):
    t, d = y_lru.shape
    tn = TN_MERGE
    col0 = (proj.shape[1] - 2 * d) // tn
    return pl.pallas_call(
        _merge_kernel,
        grid=(t // TM, d // tn),
        in_specs=[
            pl.BlockSpec((TM, d), lambda i, j: (i, 0)),
            pl.BlockSpec((TM, d), lambda i, j: (i, 0)),
            pl.BlockSpec((TM, tn), lambda i, j: (i, col0 + j)),
            pl.BlockSpec((TM, tn), lambda i, j: (i, col0 + d // tn + j)),
            pl.BlockSpec((d, tn), lambda i, j: (0, j)),
            pl.BlockSpec((d, tn), lambda i, j: (0, j)),
        ],
        out_specs=pl.BlockSpec((TM, tn), lambda i, j: (i, j)),
        out_shape=jax.ShapeDtypeStruct((t, d), BF16),
        compiler_params=_params(("arbitrary", "arbitrary"), 48),
        name="merge",
    )(y_lru, y_pool, proj, proj, wl, wp)


def _panel_kernel(a_ref, w_ref, x_ref, mod_ref, g_ref, *rest, gate_row, next_shift_row):
    if next_shift_row is None:
        o_ref, vec_ref = rest
    else:
        gn_ref, o_ref, u_ref, vec_ref = rest
    k = pl.program_id(1)
    tm, d = o_ref.shape

    def accumulate(first):
        a = a_ref[...]
        for n0 in range(0, d, PANEL_NCH):
            part = jnp.dot(a, w_ref[:, n0:n0 + PANEL_NCH], preferred_element_type=F32)
            if first:
                o_ref[:, n0:n0 + PANEL_NCH] = part
            else:
                o_ref[:, n0:n0 + PANEL_NCH] += part

    pl.when(k == 0)(functools.partial(accumulate, True))
    pl.when(k > 0)(functools.partial(accumulate, False))

    @pl.when(k == pl.num_programs(1) - 1)
    def _():
        vec_ref[0:1, :] = mod_ref[0, gate_row:gate_row + 1, :] * g_ref[...]
        if next_shift_row is not None:
            vec_ref[1:2, :] = gn_ref[...] * (1.0 + mod_ref[0, next_shift_row + 1:next_shift_row + 2, :])

        def inv_rms(v):
            return lax.rsqrt(jnp.mean(v * v, axis=-1, keepdims=True) + RMS_EPS)

        def rows(rb, carry):
            sl = pl.ds(pl.multiple_of(rb * PANEL_ROWS, PANEL_ROWS), PANEL_ROWS)
            inv = inv_rms(o_ref[sl, :])
            o_ref[sl, :] = x_ref[sl, :] + vec_ref[0:1, :] * (o_ref[sl, :] * inv)
            if next_shift_row is not None:
                inv2 = inv_rms(o_ref[sl, :])
                shift = mod_ref[0, next_shift_row:next_shift_row + 1, :]
                u_ref[sl, :] = (vec_ref[1:2, :] * (o_ref[sl, :] * inv2) + shift).astype(BF16)
            return carry

        lax.fori_loop(0, tm // PANEL_ROWS, rows, 0)


def _panel(a, w, x2d, modv, g, g_next=None, *, seq, tk, gate_row, next_shift_row=None, name):
    t, kdim = a.shape
    d = w.shape[1]
    tiles_per_seq = seq // TM
    row_spec = pl.BlockSpec((TM, d), lambda i, k: (i, 0))
    vec_spec = pl.BlockSpec((1, d), lambda i, k: (0, 0))
    in_specs = [
        pl.BlockSpec((TM, tk), lambda i, k: (i, k)),
        pl.BlockSpec((tk, d), lambda i, k: (k, 0)),
        row_spec,
        pl.BlockSpec((1, SUBLANES, d), lambda i, k: (i // tiles_per_seq, 0, 0)),
        vec_spec,
    ]
    args = [a, w, x2d, modv, g]
    out_specs = row_spec
    out_shape = jax.ShapeDtypeStruct((t, d), F32)
    if next_shift_row is not None:
        in_specs.append(vec_spec)
        args.append(g_next)
        out_specs = [row_spec, row_spec]
        out_shape = [out_shape, jax.ShapeDtypeStruct((t, d), BF16)]
    return pl.pallas_call(
        functools.partial(_panel_kernel, gate_row=gate_row, next_shift_row=next_shift_row),
        grid=(t // TM, kdim // tk),
        in_specs=in_specs,
        out_specs=out_specs,
        out_shape=out_shape,
        scratch_shapes=[pltpu.VMEM((SUBLANES, d), F32)],
        compiler_params=_params(("arbitrary", "arbitrary"), 58),
        name=name,
    )(*args)


def _layer(x2d, modv, *, batch, seq, g_mix_pre, g_mix_post, w_in, conv_w, conv_b, w_rg_a, b_rg_a,
           w_rg_x, b_rg_x, lru_lambda, pool_w, pool_scale, w_branch_lru, w_branch_pool, w_o,
           g_ffn_pre, g_ffn_post, w_gate_up, w_down):
    d = x2d.shape[1]
    row = lambda v: v.reshape(1, -1)

    proj = _inproj(_prenorm(x2d, modv, row(g_mix_pre), seq=seq), w_in.astype(BF16))
    y_lru = _lru(proj, conv_w, row(conv_b), w_rg_a.astype(BF16), row(b_rg_a),
                 w_rg_x.astype(BF16), row(b_rg_x), row(lru_lambda), batch=batch, seq=seq, d=d)
    y_pool = _pool(proj, pool_w.astype(BF16), row(pool_scale), batch=batch, seq=seq, d=d)
    merged = _merge(y_lru, y_pool, proj, w_branch_lru.astype(BF16), w_branch_pool.astype(BF16))
    x1, u2 = _panel(merged, w_o.astype(BF16), x2d, modv, row(g_mix_post), row(g_ffn_pre), seq=seq,
                    tk=TK_WO, gate_row=2, next_shift_row=3, name="wo_post")

    f = w_down.shape[0]
    assert f % TF == 0
    fp = -(-f // TK_DOWN) * TK_DOWN
    hid = _gate_up(u2, w_gate_up, f=f, fp=fp)
    w_dn = jnp.pad(w_down.astype(BF16), ((0, fp - f), (0, 0)))
    return _panel(hid, w_dn, x1, modv, row(g_ffn_post), seq=seq, tk=TK_DOWN, gate_row=5,
                  name="down_post")


def kernel(x, c, w_ada, b_ada, g_mix_pre, g_mix_post, w_in, conv_w, conv_b, w_rg_a, b_rg_a,
           w_rg_x, b_rg_x, lru_lambda, pool_w, pool_scale, w_branch_lru, w_branch_pool, w_o,
           g_ffn_pre, g_ffn_post, w_gate_up, w_down):
    batch, seq, d = x.shape
    depth = w_ada.shape[0]
    n_mod = w_ada.shape[2] // d
    assert seq % TM_GU == 0 and seq % TM_IN == 0 and seq % TL == 0 and d % TN_IN == 0
    assert batch <= SUBLANES

    c_pad = jnp.pad(c, ((0, SUBLANES - batch), (0, 0)))
    x2d = x.reshape(batch * seq, d)
    for l in range(depth):
        mod = _ada(c_pad, w_ada[l], b_ada[l].reshape(1, -1))
        modv = jnp.pad(mod[:batch].reshape(batch, n_mod, d), ((0, 0), (0, SUBLANES - n_mod), (0, 0)))
        x2d = _layer(
            x2d, modv, batch=batch, seq=seq,
            g_mix_pre=g_mix_pre[l], g_mix_post=g_mix_post[l], w_in=w_in[l], conv_w=conv_w[l],
            conv_b=conv_b[l], w_rg_a=w_rg_a[l], b_rg_a=b_rg_a[l], w_rg_x=w_rg_x[l],
            b_rg_x=b_rg_x[l], lru_lambda=lru_lambda[l], pool_w=pool_w[l],
            pool_scale=pool_scale[l], w_branch_lru=w_branch_lru[l], w_branch_pool=w_branch_pool[l],
            w_o=w_o[l], g_ffn_pre=g_ffn_pre[l], g_ffn_post=g_ffn_post[l],
            w_gate_up=w_gate_up[l], w_down=w_down[l])
    return x2d.reshape(batch, seq, d)
```
